```python
import jax, jax.numpy as jnp
from jax import lax
import numpy as np

D_MODEL = 1024
BATCH = 8
SEQ = 8192
DEPTH = 2

CHUNK = 64
GMLP_BLOCK = 128
A_HEADS = 4
A_HEAD_DIM = 128
D_A = A_HEADS * A_HEAD_DIM
B_GROUPS = 4
D_B = 512
CONV_WIDTH = 3
D_FF = 2816
N_BRANCH = 2
D_IN = 2 * D_A + 3 * D_B + N_BRANCH * D_MODEL
RMS_EPS = 1e-6
LN_EPS = 1e-5

kernel_name = "hybrid_gmlp_shortconv_gated_encoder"


def rms_norm(x, g):
    xf = x.astype(jnp.float32)
    y = xf * lax.rsqrt(jnp.mean(xf * xf, axis=-1, keepdims=True) + RMS_EPS)
    return (y * g.astype(jnp.float32)).astype(x.dtype)


def layer_norm(x, g, b):
    xf = x.astype(jnp.float32)
    mu = jnp.mean(xf, axis=-1, keepdims=True)
    var = jnp.mean(jnp.square(xf - mu), axis=-1, keepdims=True)
    y = (xf - mu) * lax.rsqrt(var + LN_EPS)
    return (y * g.astype(jnp.float32) + b.astype(jnp.float32)).astype(x.dtype)


def causal_dwconv(x, w):
    k, c = w.shape
    return lax.conv_general_dilated(
        x, w[:, None, :].astype(x.dtype), window_strides=(1,), padding=[(k - 1, 0)],
        dimension_numbers=("NWC", "WIO", "NWC"), feature_group_count=c)


def spatial_mask():
    idx = jnp.arange(GMLP_BLOCK) // CHUNK
    return idx[None, :] <= idx[:, None]


def gmlp_branch(u, v, ln_g, ln_b, w_s, b_s, mask):
    bsz, s, _ = u.shape
    u = jax.nn.gelu(u)
    v = layer_norm(jax.nn.gelu(v), ln_g, ln_b)
    vb = v.reshape(bsz, s // GMLP_BLOCK, GMLP_BLOCK, A_HEADS, A_HEAD_DIM)
    w_m = jnp.where(mask[None], w_s, jnp.zeros((), w_s.dtype))
    f = jnp.einsum("hij,bnjhd->bnihd", w_m, vb) + b_s.T[None, None, :, :, None]
    return u * f.reshape(bsz, s, D_A)


def setup_inputs(seed: int = 0) -> dict:
    key = jax.random.key(seed)
    ks = jax.random.split(key, 20)
    n = jax.random.normal
    f32 = jnp.float32
    return {
        "x": n(ks[0], (BATCH, SEQ, D_MODEL), f32),
        "norm1_g": 1.0 + 0.02 * n(ks[1], (DEPTH, D_MODEL), f32),
        "w_in": n(ks[2], (DEPTH, D_MODEL, D_IN), f32) * D_MODEL ** -0.5,
        "b_gate": 0.02 * n(ks[3], (DEPTH, N_BRANCH * D_MODEL), f32),
        "gmlp_ln_g": 1.0 + 0.02 * n(ks[4], (DEPTH, D_A), f32),
        "gmlp_ln_b": 0.02 * n(ks[5], (DEPTH, D_A), f32),
        "w_spatial": n(ks[6], (DEPTH, A_HEADS, GMLP_BLOCK, GMLP_BLOCK), f32) * (0.5 * GMLP_BLOCK ** -0.5),
        "b_spatial": 1.0 + 0.02 * n(ks[7], (DEPTH, A_HEADS, GMLP_BLOCK), f32),
        "w_shortconv": n(ks[8], (DEPTH, CONV_WIDTH, D_B), f32) * CONV_WIDTH ** -0.5,
        "w_branch": n(ks[9], (DEPTH, N_BRANCH, D_A, D_MODEL), f32) * D_A ** -0.5,
        "w_out": n(ks[10], (DEPTH, D_MODEL, D_MODEL), f32) * D_MODEL ** -0.5,
        "norm2_g": 1.0 + 0.02 * n(ks[11], (DEPTH, D_MODEL), f32),
        "w_ffn_up": n(ks[12], (DEPTH, D_MODEL, 2 * D_FF), f32) * D_MODEL ** -0.5,
        "w_ffn_conv": n(ks[13], (DEPTH, CONV_WIDTH, D_FF), f32) * CONV_WIDTH ** -0.5,
        "b_ffn_conv": 0.02 * n(ks[14], (DEPTH, D_FF), f32),
        "w_ffn_down": n(ks[15], (DEPTH, D_FF, D_MODEL), f32) * D_FF ** -0.5,
        "final_g": 1.0 + 0.02 * n(ks[16], (D_MODEL,), f32),
    }


def reference(x, norm1_g, w_in, b_gate, gmlp_ln_g, gmlp_ln_b, w_spatial, b_spatial,
              w_shortconv, w_branch, w_out, norm2_g, w_ffn_up, w_ffn_conv, b_ffn_conv,
              w_ffn_down, final_g):
    mask = spatial_mask()
    cuts = [D_A, 2 * D_A, 2 * D_A + D_B, 2 * D_A + 2 * D_B, 2 * D_A + 3 * D_B,
            2 * D_A + 3 * D_B + D_MODEL]
    for l in range(DEPTH):
        h = rms_norm(x, norm1_g[l])
        z = h @ w_in[l]
        u, v, bg, cg, hb, ga, gb = jnp.split(z, cuts, axis=-1)
        y_a = gmlp_branch(u, v, gmlp_ln_g[l], gmlp_ln_b[l], w_spatial[l], b_spatial[l], mask)
        y_b = bg * causal_dwconv(cg * hb, w_shortconv[l])
        g_a = jax.nn.sigmoid(ga + b_gate[l, :D_MODEL])
        g_b = jax.nn.sigmoid(gb + b_gate[l, D_MODEL:])
        merged = g_a * (y_a @ w_branch[l, 0]) + g_b * (y_b @ w_branch[l, 1])
        x = x + merged @ w_out[l]
        h = rms_norm(x, norm2_g[l])
        up = h @ w_ffn_up[l]
        gate, val = up[..., :D_FF], up[..., D_FF:]
        gate = causal_dwconv(gate, w_ffn_conv[l]) + b_ffn_conv[l]
        x = x + (jax.nn.silu(gate) * val) @ w_ffn_down[l]
    return rms_norm(x, final_g)
```

```python
import functools
import math

import jax
import jax.numpy as jnp
from jax import lax
from jax.experimental import pallas as pl
from jax.experimental.pallas import tpu as pltpu

CHUNK = 64
GMLP_BLOCK = 128
A_HEADS = 4
A_HEAD_DIM = 128
D_A = A_HEADS * A_HEAD_DIM
D_B = 512
CONV_WIDTH = 3
RMS_EPS = 1e-6
LN_EPS = 1e-5

V7X_SUBLANES = 8
V7X_VMEM_BYTES = 64 * 1024 * 1024

ROW_TILE = 256
VMEM_LIMIT_BYTES = 48 * 1024 * 1024

_SQRT_2_OVER_PI = math.sqrt(2.0 / math.pi)


def _rms_norm(x, g):
    return x * lax.rsqrt(jnp.mean(x * x, axis=-1, keepdims=True) + RMS_EPS) * g


def _gelu_tanh(x):
    return x * (0.5 * (1.0 + jnp.tanh(_SQRT_2_OVER_PI * (x + 0.044715 * (x * x * x)))))


def _sigmoid(x):
    return 1.0 / (1.0 + jnp.exp(-x))


def _causal_dwconv3(cur, w_ref, hist_ref, first_tile_of_seq):
    rows = cur.shape[0]

    @pl.when(first_tile_of_seq)
    def _():
        hist_ref[0:V7X_SUBLANES, :] = jnp.zeros((V7X_SUBLANES, cur.shape[1]), jnp.float32)

    hist_ref[V7X_SUBLANES:V7X_SUBLANES + rows, :] = cur
    out = (w_ref[0:1, :] * hist_ref[V7X_SUBLANES - 2:V7X_SUBLANES - 2 + rows, :]
           + w_ref[1:2, :] * hist_ref[V7X_SUBLANES - 1:V7X_SUBLANES - 1 + rows, :]
           + w_ref[2:3, :] * cur)
    hist_ref[0:V7X_SUBLANES, :] = hist_ref[rows:rows + V7X_SUBLANES, :]
    return out


def _mixer_kernel(x_ref, g1_ref, w_in_ref, bgate_ref, lng_ref, lnb_ref, wsp_ref, bsp_ref,
                  wconv_ref, pa_ref, pb_ref, wout_ref, o_ref, hist_ref, *, tiles_per_seq):
    rows, d_model = x_ref.shape
    first_tile_of_seq = pl.program_id(0) % tiles_per_seq == 0

    x = x_ref[...]
    h = _rms_norm(x, g1_ref[...]).astype(jnp.bfloat16)
    z = jnp.dot(h, w_in_ref[...], preferred_element_type=jnp.float32)

    u = _gelu_tanh(z[:, 0:D_A])
    v = _gelu_tanh(z[:, D_A:2 * D_A])
    mu = jnp.mean(v, axis=-1, keepdims=True)
    vc = v - mu
    var = jnp.mean(vc * vc, axis=-1, keepdims=True)
    vn = (vc * lax.rsqrt(var + LN_EPS) * lng_ref[...] + lnb_ref[...]).astype(jnp.bfloat16)

    qi = lax.broadcasted_iota(jnp.int32, (GMLP_BLOCK, GMLP_BLOCK), 0) // CHUNK
    kj = lax.broadcasted_iota(jnp.int32, (GMLP_BLOCK, GMLP_BLOCK), 1) // CHUNK
    causal = kj <= qi
    w_heads = [jnp.where(causal, wsp_ref[hd], 0.0).astype(jnp.bfloat16) for hd in range(A_HEADS)]
    f_blocks = []
    for n in range(rows // GMLP_BLOCK):
        r0 = n * GMLP_BLOCK
        f_heads = [
            jnp.dot(w_heads[hd], vn[r0:r0 + GMLP_BLOCK, hd * A_HEAD_DIM:(hd + 1) * A_HEAD_DIM],
                    preferred_element_type=jnp.float32)
            for hd in range(A_HEADS)
        ]
        f_blocks.append(jnp.concatenate(f_heads, axis=1) + bsp_ref[...])
    y_a = (u * jnp.concatenate(f_blocks, axis=0)).astype(jnp.bfloat16)

    o_b = 2 * D_A
    bg = z[:, o_b:o_b + D_B]
    p = z[:, o_b + D_B:o_b + 2 * D_B] * z[:, o_b + 2 * D_B:o_b + 3 * D_B]
    y_b = (bg * _causal_dwconv3(p, wconv_ref, hist_ref, first_tile_of_seq)).astype(jnp.bfloat16)

    o_g = o_b + 3 * D_B
    g_a = _sigmoid(z[:, o_g:o_g + d_model] + bgate_ref[:, 0:d_model])
    g_b = _sigmoid(z[:, o_g + d_model:o_g + 2 * d_model] + bgate_ref[:, d_model:2 * d_model])
    merged = (g_a * jnp.dot(y_a, pa_ref[...], preferred_element_type=jnp.float32)
              + g_b * jnp.dot(y_b, pb_ref[...], preferred_element_type=jnp.float32))
    o_ref[...] = x + jnp.dot(merged.astype(jnp.bfloat16), wout_ref[...],
                             preferred_element_type=jnp.float32)


def _ffn_kernel(x_ref, g2_ref, wup_ref, wconv_ref, bconv_ref, wdown_ref, gfin_ref, o_ref, hist_ref,
                *, tiles_per_seq, apply_final_norm):
    d_ff = wdown_ref.shape[0]
    first_tile_of_seq = pl.program_id(0) % tiles_per_seq == 0

    x = x_ref[...]
    h = _rms_norm(x, g2_ref[...]).astype(jnp.bfloat16)
    up = jnp.dot(h, wup_ref[...], preferred_element_type=jnp.float32)
    gate = _causal_dwconv3(up[:, 0:d_ff], wconv_ref, hist_ref, first_tile_of_seq) + bconv_ref[...]
    act = (gate * _sigmoid(gate) * up[:, d_ff:2 * d_ff]).astype(jnp.bfloat16)
    y = x + jnp.dot(act, wdown_ref[...], preferred_element_type=jnp.float32)
    if apply_final_norm:
        y = _rms_norm(y, gfin_ref[...])
    o_ref[...] = y


def _resident(shape):
    return pl.BlockSpec(shape, lambda i: (0,) * len(shape), pipeline_mode=pl.Buffered(1))


def _row_tiled_call(body, x2d, params, hist_cols):
    n_rows, d_model = x2d.shape
    assert n_rows % ROW_TILE == 0 and ROW_TILE % GMLP_BLOCK == 0
    row_spec = pl.BlockSpec((ROW_TILE, d_model), lambda i: (i, 0))
    return pl.pallas_call(
        body,
        out_shape=jax.ShapeDtypeStruct(x2d.shape, x2d.dtype),
        grid=(n_rows // ROW_TILE,),
        in_specs=[row_spec] + [_resident(p.shape) for p in params],
        out_specs=row_spec,
        scratch_shapes=[pltpu.VMEM((ROW_TILE + V7X_SUBLANES, hist_cols), jnp.float32)],
        compiler_params=pltpu.CompilerParams(
            dimension_semantics=("arbitrary",), vmem_limit_bytes=VMEM_LIMIT_BYTES),
    )(x2d, *params)


def kernel(x, norm1_g, w_in, b_gate, gmlp_ln_g, gmlp_ln_b, w_spatial, b_spatial, w_shortconv,
           w_branch, w_out, norm2_g, w_ffn_up, w_ffn_conv, b_ffn_conv, w_ffn_down, final_g):
    batch, seq, d_model = x.shape
    depth = w_in.shape[0]
    assert seq % ROW_TILE == 0
    tiles_per_seq = seq // ROW_TILE
    bf16 = jnp.bfloat16
    row = lambda a: a.reshape(1, -1)

    x2d = x.reshape(batch * seq, d_model)
    for l in range(depth):
        bsp = jnp.repeat(b_spatial[l].T, A_HEAD_DIM, axis=1)
        mixer_params = (row(norm1_g[l]), w_in[l].astype(bf16), row(b_gate[l]), row(gmlp_ln_g[l]),
                        row(gmlp_ln_b[l]), w_spatial[l], bsp, w_shortconv[l],
                        w_branch[l, 0].astype(bf16), w_branch[l, 1].astype(bf16),
                        w_out[l].astype(bf16))
        x2d = _row_tiled_call(functools.partial(_mixer_kernel, tiles_per_seq=tiles_per_seq),
                              x2d, mixer_params, D_B)
        ffn_params = (row(norm2_g[l]), w_ffn_up[l].astype(bf16), w_ffn_conv[l], row(b_ffn_conv[l]),
                      w_ffn_down[l].astype(bf16), row(final_g))
        x2d = _row_tiled_call(
            functools.partial(_ffn_kernel, tiles_per_seq=tiles_per_seq,
                              apply_final_norm=(l == depth - 1)),
            x2d, ffn_params, w_ffn_down.shape[1])
    return x2d.reshape(batch, seq, d_model)
```

```python
import functools
import math

import jax
import jax.numpy as jnp
from jax import lax
from jax.experimental import pallas as pl
from jax.experimental.pallas import tpu as pltpu

CHUNK = 64
GMLP_BLOCK = 128
A_HEADS = 4
A_HEAD_DIM = 128
D_A = A_HEADS * A_HEAD_DIM
D_B = 512
CONV_WIDTH = 3
RMS_EPS = 1e-6
LN_EPS = 1e-5

V7X_SUBLANES = 8
V7X_MXU_COLS = 256

ROW_TILE = 512
COL_CHUNK = V7X_MXU_COLS
VMEM_LIMIT_BYTES = 56 * 1024 * 1024

_SQRT_2_OVER_PI = math.sqrt(2.0 / math.pi)


def _rms_norm(x, g):
    return x * lax.rsqrt(jnp.mean(x * x, axis=-1, keepdims=True) + RMS_EPS) * g


def _gelu_tanh(x):
    return x * (0.5 * (1.0 + jnp.tanh(_SQRT_2_OVER_PI * (x + 0.044715 * (x * x * x)))))


def _sigmoid(x):
    return 1.0 / (1.0 + jnp.exp(-x))


def _dot(a, b):
    return jnp.dot(a, b, preferred_element_type=jnp.float32)


def _causal_dwconv3(cur, w, hist_ref):
    rows = cur.shape[0]
    hist_ref[V7X_SUBLANES:V7X_SUBLANES + rows, :] = cur
    return (w[0:1, :] * hist_ref[V7X_SUBLANES - 2:V7X_SUBLANES - 2 + rows, :]
            + w[1:2, :] * hist_ref[V7X_SUBLANES - 1:V7X_SUBLANES - 1 + rows, :]
            + w[2:3, :] * cur)


def _reset_history(hist_ref, first_tile_of_seq):
    @pl.when(first_tile_of_seq)
    def _():
        hist_ref[0:V7X_SUBLANES, :] = jnp.zeros((V7X_SUBLANES, hist_ref.shape[1]), jnp.float32)


def _carry_history(hist_ref, rows):
    hist_ref[0:V7X_SUBLANES, :] = hist_ref[rows:rows + V7X_SUBLANES, :]


def _mixer_kernel(x_ref, g1_ref, wuv_ref, wbch_ref, wgate_ref, bgate_ref, lng_ref, lnb_ref, wsp_ref,
                  bsp_ref, wconv_ref, pa_ref, pb_ref, wout_ref, o_ref, hist_ref, merged_ref,
                  *, tiles_per_seq):
    rows, d_model = x_ref.shape
    _reset_history(hist_ref, pl.program_id(0) % tiles_per_seq == 0)

    x = x_ref[...]
    h = _rms_norm(x, g1_ref[...]).astype(jnp.bfloat16)

    z_uv = _dot(h, wuv_ref[...])
    u = _gelu_tanh(z_uv[:, 0:D_A])
    v = _gelu_tanh(z_uv[:, D_A:2 * D_A])
    mu = jnp.mean(v, axis=-1, keepdims=True)
    vc = v - mu
    var = jnp.mean(vc * vc, axis=-1, keepdims=True)
    vn = (vc * lax.rsqrt(var + LN_EPS) * lng_ref[...] + lnb_ref[...]).astype(jnp.bfloat16)

    qi = lax.broadcasted_iota(jnp.int32, (GMLP_BLOCK, GMLP_BLOCK), 0) // CHUNK
    kj = lax.broadcasted_iota(jnp.int32, (GMLP_BLOCK, GMLP_BLOCK), 1) // CHUNK
    causal = kj <= qi
    w_heads = [jnp.where(causal, wsp_ref[hd], 0.0).astype(jnp.bfloat16) for hd in range(A_HEADS)]
    f_blocks = []
    for n in range(rows // GMLP_BLOCK):
        r0 = n * GMLP_BLOCK
        f_heads = [_dot(w_heads[hd], vn[r0:r0 + GMLP_BLOCK, hd * A_HEAD_DIM:(hd + 1) * A_HEAD_DIM])
                   for hd in range(A_HEADS)]
        f_blocks.append(jnp.concatenate(f_heads, axis=1) + bsp_ref[...])
    y_a = (u * jnp.concatenate(f_blocks, axis=0)).astype(jnp.bfloat16)

    z_b = _dot(h, wbch_ref[...])
    p = z_b[:, D_B:2 * D_B] * z_b[:, 2 * D_B:3 * D_B]
    y_b = (z_b[:, 0:D_B] * _causal_dwconv3(p, wconv_ref[...], hist_ref)).astype(jnp.bfloat16)
    _carry_history(hist_ref, rows)

    cw = wgate_ref.shape[2] // 2
    for c in range(wgate_ref.shape[0]):
        cols = slice(c * cw, (c + 1) * cw)
        zg = _dot(h, wgate_ref[c])
        g_a = _sigmoid(zg[:, 0:cw] + bgate_ref[:, cols])
        g_b = _sigmoid(zg[:, cw:2 * cw] + bgate_ref[:, d_model + c * cw:d_model + (c + 1) * cw])
        merged = g_a * _dot(y_a, pa_ref[:, cols]) + g_b * _dot(y_b, pb_ref[:, cols])
        merged_ref[:, cols] = merged.astype(jnp.bfloat16)
    o_ref[...] = x + _dot(merged_ref[...], wout_ref[...])


def _ffn_kernel(x_ref, g2_ref, wup_ref, wconv_ref, bconv_ref, wdown_ref, gfin_ref, o_ref, hist_ref,
                act_ref, *, tiles_per_seq, apply_final_norm):
    rows = x_ref.shape[0]
    _reset_history(hist_ref, pl.program_id(0) % tiles_per_seq == 0)

    x = x_ref[...]
    h = _rms_norm(x, g2_ref[...]).astype(jnp.bfloat16)
    cw = wup_ref.shape[2] // 2
    for c in range(wup_ref.shape[0]):
        cols = slice(c * cw, (c + 1) * cw)
        gv = _dot(h, wup_ref[c])
        gate = _causal_dwconv3(gv[:, 0:cw], wconv_ref[:, cols], hist_ref.at[:, cols]) + bconv_ref[:, cols]
        act_ref[:, cols] = (gate * _sigmoid(gate) * gv[:, cw:2 * cw]).astype(jnp.bfloat16)
    _carry_history(hist_ref, rows)
    y = x + _dot(act_ref[...], wdown_ref[...])
    if apply_final_norm:
        y = _rms_norm(y, gfin_ref[...])
    o_ref[...] = y


def _resident(shape):
    return pl.BlockSpec(shape, lambda i: (0,) * len(shape), pipeline_mode=pl.Buffered(1))


def _row_tiled_call(body, x2d, params, hist_cols, staged_cols):
    n_rows, d_model = x2d.shape
    assert n_rows % ROW_TILE == 0 and ROW_TILE % GMLP_BLOCK == 0
    row_spec = pl.BlockSpec((ROW_TILE, d_model), lambda i: (i, 0))
    return pl.pallas_call(
        body,
        out_shape=jax.ShapeDtypeStruct(x2d.shape, x2d.dtype),
        grid=(n_rows // ROW_TILE,),
        in_specs=[row_spec] + [_resident(p.shape) for p in params],
        out_specs=row_spec,
        scratch_shapes=[pltpu.VMEM((ROW_TILE + V7X_SUBLANES, hist_cols), jnp.float32),
                        pltpu.VMEM((ROW_TILE, staged_cols), jnp.bfloat16)],
        compiler_params=pltpu.CompilerParams(
            dimension_semantics=("arbitrary",), vmem_limit_bytes=VMEM_LIMIT_BYTES),
    )(x2d, *params)


def _paired_col_chunks(w_left, w_right):
    k, n = w_left.shape
    assert n % COL_CHUNK == 0
    split = lambda w: w.reshape(k, n // COL_CHUNK, COL_CHUNK).transpose(1, 0, 2)
    return jnp.concatenate([split(w_left), split(w_right)], axis=2)


def kernel(x, norm1_g, w_in, b_gate, gmlp_ln_g, gmlp_ln_b, w_spatial, b_spatial, w_shortconv,
           w_branch, w_out, norm2_g, w_ffn_up, w_ffn_conv, b_ffn_conv, w_ffn_down, final_g):
    batch, seq, d_model = x.shape
    depth = w_in.shape[0]
    d_ff = w_ffn_down.shape[1]
    assert seq % ROW_TILE == 0
    tiles_per_seq = seq // ROW_TILE
    bf16 = jnp.bfloat16
    row = lambda a: a.reshape(1, -1)
    o_b, o_g = 2 * D_A, 2 * D_A + 3 * D_B

    x2d = x.reshape(batch * seq, d_model)
    for l in range(depth):
        w_in_l = w_in[l].astype(bf16)
        bsp = jnp.repeat(b_spatial[l].T, A_HEAD_DIM, axis=1)
        mixer_params = (row(norm1_g[l]), w_in_l[:, 0:o_b], w_in_l[:, o_b:o_g],
                        _paired_col_chunks(w_in_l[:, o_g:o_g + d_model], w_in_l[:, o_g + d_model:]),
                        row(b_gate[l]), row(gmlp_ln_g[l]), row(gmlp_ln_b[l]), w_spatial[l], bsp,
                        w_shortconv[l], w_branch[l, 0].astype(bf16), w_branch[l, 1].astype(bf16),
                        w_out[l].astype(bf16))
        x2d = _row_tiled_call(functools.partial(_mixer_kernel, tiles_per_seq=tiles_per_seq),
                              x2d, mixer_params, D_B, d_model)
        w_up_l = w_ffn_up[l].astype(bf16)
        ffn_params = (row(norm2_g[l]), _paired_col_chunks(w_up_l[:, 0:d_ff], w_up_l[:, d_ff:]),
                      w_ffn_conv[l], row(b_ffn_conv[l]), w_ffn_down[l].astype(bf16), row(final_g))
        x2d = _row_tiled_call(
            functools.partial(_ffn_kernel, tiles_per_seq=tiles_per_seq,
                              apply_final_norm=(l == depth - 1)),
            x2d, ffn_params, d_ff, d_ff)
    return x2d.reshape(batch, seq, d_model)
```

```python
import functools
import math

import jax
import jax.numpy as jnp
from jax import lax
from jax.experimental import pallas as pl
from jax.experimental.pallas import tpu as pltpu

CHUNK = 64
GMLP_BLOCK = 128
A_HEADS = 4
A_HEAD_DIM = 128
D_A = A_HEADS * A_HEAD_DIM
D_B = 512
CONV_WIDTH = 3
RMS_EPS = 1e-6
LN_EPS = 1e-5

V7X_SUBLANES = 8
V7X_MXU_COLS = 256

ROW_TILE = 512
COL_CHUNK = V7X_MXU_COLS
VMEM_LIMIT_BYTES = 56 * 1024 * 1024

_SQRT_2_OVER_PI = math.sqrt(2.0 / math.pi)


def _rms_norm(x, g):
    return x * lax.rsqrt(jnp.mean(x * x, axis=-1, keepdims=True) + RMS_EPS) * g


def _gelu_tanh(x):
    return x * (0.5 * (1.0 + jnp.tanh(_SQRT_2_OVER_PI * (x + 0.044715 * (x * x * x)))))


def _sigmoid(x):
    return 1.0 / (1.0 + jnp.exp(-x))


def _dot(a, b):
    return jnp.dot(a, b, preferred_element_type=jnp.float32)


def _causal_dwconv3(cur, w, hist_ref):
    rows = cur.shape[0]
    hist_ref[V7X_SUBLANES:V7X_SUBLANES + rows, :] = cur
    return (w[0:1, :] * hist_ref[V7X_SUBLANES - 2:V7X_SUBLANES - 2 + rows, :]
            + w[1:2, :] * hist_ref[V7X_SUBLANES - 1:V7X_SUBLANES - 1 + rows, :]
            + w[2:3, :] * cur)


def _reset_history(hist_ref, first_tile_of_seq):
    @pl.when(first_tile_of_seq)
    def _():
        hist_ref[0:V7X_SUBLANES, :] = jnp.zeros((V7X_SUBLANES, hist_ref.shape[1]), jnp.float32)


def _carry_history(hist_ref, rows):
    hist_ref[0:V7X_SUBLANES, :] = hist_ref[rows:rows + V7X_SUBLANES, :]


def _first_step_norm(h_ref, x_ref, g_ref):
    @pl.when(pl.program_id(0) == 0)
    def _():
        h_ref[...] = _rms_norm(x_ref[...], g_ref[...]).astype(h_ref.dtype)


def _mixer_kernel(x_ref, xnext_ref, g1_ref, win_ref, bgate_ref, lng_ref, lnb_ref, wsp_ref, bsp_ref,
                  wconv_ref, pa_ref, pb_ref, wout_ref, o_ref, h_ref, hist_ref, merged_ref,
                  *, tiles_per_seq):
    rows, d_model = x_ref.shape
    o_b, o_g = 2 * D_A, 2 * D_A + 3 * D_B
    cw = COL_CHUNK
    n_chunks = d_model // cw
    _reset_history(hist_ref, pl.program_id(0) % tiles_per_seq == 0)
    _first_step_norm(h_ref, x_ref, g1_ref)

    def h_dot(lo, width):
        return _dot(h_ref[...], win_ref[:, lo:lo + width])

    def gate_dots(c):
        return h_dot(o_g + c * cw, cw), h_dot(o_g + d_model + c * cw, cw)

    z_v = h_dot(D_A, D_A)
    z_u = h_dot(0, D_A)
    v = _gelu_tanh(z_v)
    mu = jnp.mean(v, axis=-1, keepdims=True)
    vc = v - mu
    var = jnp.mean(vc * vc, axis=-1, keepdims=True)
    vn = (vc * lax.rsqrt(var + LN_EPS) * lng_ref[...] + lnb_ref[...]).astype(jnp.bfloat16)
    z_b = h_dot(o_b, 3 * D_B)

    qi = lax.broadcasted_iota(jnp.int32, (GMLP_BLOCK, GMLP_BLOCK), 0) // CHUNK
    kj = lax.broadcasted_iota(jnp.int32, (GMLP_BLOCK, GMLP_BLOCK), 1) // CHUNK
    causal = kj <= qi
    w_heads = [jnp.where(causal, wsp_ref[hd], 0.0).astype(jnp.bfloat16) for hd in range(A_HEADS)]
    f_blocks = []
    for n in range(rows // GMLP_BLOCK):
        r0 = n * GMLP_BLOCK
        f_heads = [_dot(w_heads[hd], vn[r0:r0 + GMLP_BLOCK, hd * A_HEAD_DIM:(hd + 1) * A_HEAD_DIM])
                   for hd in range(A_HEADS)]
        f_blocks.append(jnp.concatenate(f_heads, axis=1) + bsp_ref[...])
    zg = gate_dots(0)
    y_a = (_gelu_tanh(z_u) * jnp.concatenate(f_blocks, axis=0)).astype(jnp.bfloat16)

    p = z_b[:, D_B:2 * D_B] * z_b[:, 2 * D_B:3 * D_B]
    y_b = (z_b[:, 0:D_B] * _causal_dwconv3(p, wconv_ref[...], hist_ref)).astype(jnp.bfloat16)
    _carry_history(hist_ref, rows)

    for c in range(n_chunks):
        cols = slice(c * cw, (c + 1) * cw)
        zg_a, zg_b = zg
        if c + 1 < n_chunks:
            zg = gate_dots(c + 1)
        else:
            h_ref[...] = _rms_norm(xnext_ref[...], g1_ref[...]).astype(h_ref.dtype)
        g_a = _sigmoid(zg_a + bgate_ref[:, cols])
        g_b = _sigmoid(zg_b + bgate_ref[:, d_model + c * cw:d_model + (c + 1) * cw])
        merged = g_a * _dot(y_a, pa_ref[:, cols]) + g_b * _dot(y_b, pb_ref[:, cols])
        merged_ref[:, cols] = merged.astype(jnp.bfloat16)
    o_ref[...] = x_ref[...] + _dot(merged_ref[...], wout_ref[...])


def _ffn_kernel(x_ref, xnext_ref, g2_ref, wup_ref, wconv_ref, bconv_ref, wdown_ref, gfin_ref, o_ref,
                h_ref, hist_ref, act_ref, *, tiles_per_seq, apply_final_norm):
    rows = x_ref.shape[0]
    d_ff = wdown_ref.shape[0]
    cw = COL_CHUNK
    _reset_history(hist_ref, pl.program_id(0) % tiles_per_seq == 0)
    _first_step_norm(h_ref, x_ref, g2_ref)

    for c in range(d_ff // cw):
        cols = slice(c * cw, (c + 1) * cw)
        g = _dot(h_ref[...], wup_ref[:, cols])
        val = _dot(h_ref[...], wup_ref[:, d_ff + c * cw:d_ff + (c + 1) * cw])
        gate = _causal_dwconv3(g, wconv_ref[:, cols], hist_ref.at[:, cols]) + bconv_ref[:, cols]
        act_ref[:, cols] = (gate * _sigmoid(gate) * val).astype(jnp.bfloat16)
    _carry_history(hist_ref, rows)
    h_ref[...] = _rms_norm(xnext_ref[...], g2_ref[...]).astype(h_ref.dtype)

    half = rows // 2
    for r0 in (0, half):
        y = x_ref[r0:r0 + half, :] + _dot(act_ref[r0:r0 + half, :], wdown_ref[...])
        if apply_final_norm:
            y = _rms_norm(y, gfin_ref[...])
        o_ref[r0:r0 + half, :] = y


def _resident(shape, layer):
    rest = shape[1:]
    return pl.BlockSpec((None,) + rest, lambda i: (layer,) + (0,) * len(rest),
                        pipeline_mode=pl.Buffered(1))


def _row_tiled_call(body, x2d, layer, params, hist_cols, staged_cols):
    n_rows, d_model = x2d.shape
    assert n_rows % ROW_TILE == 0 and ROW_TILE % GMLP_BLOCK == 0
    n_tiles = n_rows // ROW_TILE
    row_spec = pl.BlockSpec((ROW_TILE, d_model), lambda i: (i, 0))
    next_row_spec = pl.BlockSpec((ROW_TILE, d_model), lambda i: (jnp.minimum(i + 1, n_tiles - 1), 0))
    return pl.pallas_call(
        body,
        out_shape=jax.ShapeDtypeStruct(x2d.shape, x2d.dtype),
        grid=(n_tiles,),
        in_specs=[row_spec, next_row_spec] + [_resident(p.shape, layer) for p in params],
        out_specs=row_spec,
        scratch_shapes=[pltpu.VMEM((ROW_TILE, d_model), jnp.bfloat16),
                        pltpu.VMEM((ROW_TILE + V7X_SUBLANES, hist_cols), jnp.float32),
                        pltpu.VMEM((ROW_TILE, staged_cols), jnp.bfloat16)],
        compiler_params=pltpu.CompilerParams(
            dimension_semantics=("arbitrary",), vmem_limit_bytes=VMEM_LIMIT_BYTES),
    )(x2d, x2d, *params)


def kernel(x, norm1_g, w_in, b_gate, gmlp_ln_g, gmlp_ln_b, w_spatial, b_spatial, w_shortconv,
           w_branch, w_out, norm2_g, w_ffn_up, w_ffn_conv, b_ffn_conv, w_ffn_down, final_g):
    batch, seq, d_model = x.shape
    depth = w_in.shape[0]
    d_ff = w_ffn_down.shape[1]
    assert seq % ROW_TILE == 0 and d_model % COL_CHUNK == 0 and d_ff % COL_CHUNK == 0
    tiles_per_seq = seq // ROW_TILE
    bf16 = jnp.bfloat16
    rows_of = lambda a: a.reshape(depth, 1, -1)

    bsp = jnp.repeat(jnp.swapaxes(b_spatial, 1, 2), A_HEAD_DIM, axis=2)
    mixer_params = (rows_of(norm1_g), w_in.astype(bf16), rows_of(b_gate), rows_of(gmlp_ln_g),
                    rows_of(gmlp_ln_b), w_spatial, bsp, w_shortconv,
                    w_branch[:, 0].astype(bf16), w_branch[:, 1].astype(bf16), w_out.astype(bf16))
    final_g_per_layer = jnp.broadcast_to(final_g.reshape(1, 1, -1), (depth, 1, d_model))
    ffn_params = (rows_of(norm2_g), w_ffn_up.astype(bf16), w_ffn_conv, rows_of(b_ffn_conv),
                  w_ffn_down.astype(bf16), final_g_per_layer)

    x2d = x.reshape(batch * seq, d_model)
    for l in range(depth):
        x2d = _row_tiled_call(functools.partial(_mixer_kernel, tiles_per_seq=tiles_per_seq),
                              x2d, l, mixer_params, D_B, d_model)
        x2d = _row_tiled_call(
            functools.partial(_ffn_kernel, tiles_per_seq=tiles_per_seq,
                              apply_final_norm=(l == depth - 1)),
            x2d, l, ffn_params, d_ff, d_ff)
    return x2d.reshape(batch, seq, d_model)
```
